```python
import math
import jax
import jax.numpy as jnp
from jax import lax
import numpy as np

D_MODEL = 1024
BATCH = 8
SEQ = 4096
DEPTH = 1

GRID_W = 64
CTX_LEN = 256
MIX_WIDTH = D_MODEL
FOURIER_HEAD_DIM = 64
FOURIER_WIDTH = MIX_WIDTH // 4
FOURIER_HEADS = FOURIER_WIDTH // FOURIER_HEAD_DIM
SSD_WIDTH = MIX_WIDTH - FOURIER_WIDTH
SSD_HEAD_DIM = 64
SSD_HEADS = SSD_WIDTH // SSD_HEAD_DIM
SSD_GROUPS = 4
SSD_STATE = 128
GROUP_STATE = SSD_GROUPS * SSD_STATE
CONV_WIDTH = 3
CHUNK = 128
D_FF = 2816
N_MOD = 9
MACARON_WEIGHT = 0.5
EPS = 1e-6
CONV_DIM = GROUP_STATE + SSD_WIDTH + GROUP_STATE
PROJ_WIDTH = FOURIER_WIDTH + SSD_WIDTH + CONV_DIM + 2 * SSD_HEADS

kernel_name = 'hybrid_fourier_ssd_dit_layer'


def rms_norm(x, g):
    x32 = x.astype(jnp.float32)
    y = x32 * lax.rsqrt(jnp.mean(x32 * x32, axis=-1, keepdims=True) + EPS)
    return (y * g.astype(jnp.float32)).astype(x.dtype)


def modulate(h, shift, scale):
    return h * (1 + scale[:, None, :]) + shift[:, None, :]


def macaron_ffn(x, mod3, norm_pre, norm_post, w_gate, w_up, w_down):
    h = modulate(rms_norm(x, norm_pre), mod3[:, 0], mod3[:, 1])
    y = (jax.nn.silu(h @ w_gate) * (h @ w_up)) @ w_down
    return x + MACARON_WEIGHT * mod3[:, 2][:, None, :] * rms_norm(y, norm_post)


def centred_dwconv_silu(u, w, b):
    out = lax.conv_general_dilated(u, w[:, None, :], (1,), 'SAME',
                                   dimension_numbers=('NWC', 'WIO', 'NWC'),
                                   feature_group_count=u.shape[-1])
    return jax.nn.silu(out + b)


def softplus_dt(dt_raw, dt_bias):
    b, L = dt_raw.shape[:2]
    return jax.nn.softplus(dt_raw.astype(jnp.float32).reshape(b, L, 2, SSD_HEADS)
                           + dt_bias.astype(jnp.float32))


def flip(t):
    return jnp.flip(t, axis=1)


def fourier_mix(u, fourier_w):
    b, L, _ = u.shape
    uh = u.reshape(b, L, FOURIER_HEADS, FOURIER_HEAD_DIM).astype(jnp.float32)
    f = jnp.fft.fft2(uh, axes=(1, 3), norm='ortho').real.astype(u.dtype)
    return jnp.einsum('blhd,hde->blhe', f, fourier_w).reshape(b, L, FOURIER_WIDTH)


def ssd_chunked(xs, dt, a, bm, cm, h0):
    b, L = xs.shape[:2]
    nc = L // CHUNK
    R = SSD_HEADS // SSD_GROUPS
    xdt = (xs * dt[..., None]).reshape(b, nc, CHUNK, SSD_GROUPS, R, SSD_HEAD_DIM)
    a_cs = jnp.cumsum(jnp.transpose((dt * a).reshape(b, nc, CHUNK, SSD_GROUPS, R), (0, 1, 3, 4, 2)), axis=-1)
    bc = bm.reshape(b, nc, CHUNK, SSD_GROUPS, SSD_STATE)
    cc = cm.reshape(b, nc, CHUNK, SSD_GROUPS, SSD_STATE)
    lower = jnp.tril(jnp.ones((CHUNK, CHUNK), dtype=bool))
    seg = jnp.exp(jnp.where(lower, a_cs[..., :, None] - a_cs[..., None, :], -jnp.inf))
    cb = jnp.einsum('bclgn,bcsgn->bcgls', cc, bc)
    y_diag = jnp.einsum('bcgls,bcgrls,bcsgrp->bclgrp', cb, seg, xdt)
    states = jnp.einsum('bcsgn,bcgrs,bcsgrp->bcgrpn', bc, jnp.exp(a_cs[..., -1:] - a_cs), xdt)

    def step(h, inp):
        s, d = inp
        return h * d[..., None, None] + s, h

    h_last, h_prev = lax.scan(step, h0.reshape(b, SSD_GROUPS, R, SSD_HEAD_DIM, SSD_STATE),
                              (jnp.moveaxis(states, 1, 0), jnp.moveaxis(jnp.exp(a_cs[..., -1]), 1, 0)))
    y_off = jnp.einsum('bclgn,cbgrpn,bcgrl->bclgrp', cc, h_prev, jnp.exp(a_cs))
    y = (y_diag + y_off).reshape(b, L, SSD_HEADS, SSD_HEAD_DIM)
    return y, h_last.reshape(b, SSD_HEADS, SSD_HEAD_DIM, SSD_STATE)


def ssd_final_state(xs, dt, a, bm):
    b, L = xs.shape[:2]
    R = SSD_HEADS // SSD_GROUPS
    cs = jnp.cumsum(dt * a, axis=1)
    xdt = (xs * (dt * jnp.exp(cs[:, -1:] - cs))[..., None]).reshape(b, L, SSD_GROUPS, R, SSD_HEAD_DIM)
    return jnp.einsum('blgn,blgrp->bgrpn', bm, xdt).reshape(b, SSD_HEADS, SSD_HEAD_DIM, SSD_STATE)


def mix_sublayer(x, mod3, h0_f, h0_b, norm_pre, norm_post, w_in, fourier_w, conv_w, conv_b,
                 dt_bias, a_log, d_skip, ssd_norm, w_out):
    b, L, _ = x.shape
    h = modulate(rms_norm(x, norm_pre), mod3[:, 0], mod3[:, 1])
    u_f, z, xbc, dt_raw = jnp.split(h @ w_in, [FOURIER_WIDTH, FOURIER_WIDTH + SSD_WIDTH,
                                               FOURIER_WIDTH + SSD_WIDTH + CONV_DIM], axis=-1)
    xbc = centred_dwconv_silu(xbc, conv_w, conv_b)
    cm, xs, bm = jnp.split(xbc, [GROUP_STATE, GROUP_STATE + SSD_WIDTH], axis=-1)
    xs = xs.reshape(b, L, SSD_HEADS, SSD_HEAD_DIM)
    bm = bm.reshape(b, L, SSD_GROUPS, SSD_STATE)
    cm = cm.reshape(b, L, SSD_GROUPS, SSD_STATE)
    dt = softplus_dt(dt_raw, dt_bias)
    a = -jnp.exp(a_log.astype(jnp.float32))
    y_f, h_f = ssd_chunked(xs, dt[:, :, 0], a[0], bm, cm, h0_f)
    y_b, h_b = ssd_chunked(flip(xs), flip(dt[:, :, 1]), a[1], flip(bm), flip(cm), h0_b)
    y = y_f + flip(y_b) + d_skip.astype(jnp.float32)[:, None] * xs
    y = y.reshape(b, L, SSD_WIDTH).astype(x.dtype) * jax.nn.silu(z)
    y = rms_norm(y.reshape(b, L, SSD_GROUPS, SSD_WIDTH // SSD_GROUPS),
                 ssd_norm.reshape(SSD_GROUPS, SSD_WIDTH // SSD_GROUPS)).reshape(b, L, SSD_WIDTH)
    out = jnp.concatenate([fourier_mix(u_f, fourier_w), y], axis=-1) @ w_out
    return x + mod3[:, 2][:, None, :] * rms_norm(out, norm_post), h_f, h_b


def context_states(ctx, mod3, norm_pre, w_in, conv_w, conv_b, dt_bias, a_log):
    b, L, _ = ctx.shape
    h = modulate(rms_norm(ctx, norm_pre), mod3[:, 0], mod3[:, 1])
    xb, dt_raw = jnp.split(h @ w_in[:, FOURIER_WIDTH + SSD_WIDTH + GROUP_STATE:],
                           [SSD_WIDTH + GROUP_STATE], axis=-1)
    xb = centred_dwconv_silu(xb, conv_w[:, GROUP_STATE:], conv_b[GROUP_STATE:])
    xs, bm = jnp.split(xb, [SSD_WIDTH], axis=-1)
    xs = xs.reshape(b, L, SSD_HEADS, SSD_HEAD_DIM)
    bm = bm.reshape(b, L, SSD_GROUPS, SSD_STATE)
    dt = softplus_dt(dt_raw, dt_bias)
    a = -jnp.exp(a_log.astype(jnp.float32))
    h_f = ssd_final_state(xs, dt[:, :, 0], a[0], bm)
    h_b = ssd_final_state(flip(xs), flip(dt[:, :, 1]), a[1], flip(bm))
    return h_f, h_b


def setup_inputs(seed: int = 0) -> dict:
    key = jax.random.key(seed)
    ks = iter(jax.random.split(key, 32))
    f32 = jnp.float32
    L = DEPTH

    def normal(shape, scale=1.0):
        return scale * jax.random.normal(next(ks), shape, f32)

    def gain(shape):
        return 1.0 + 0.1 * jax.random.normal(next(ks), shape, f32)

    d = {}
    d['x'] = normal((BATCH, SEQ, D_MODEL))
    d['c'] = normal((BATCH, D_MODEL))
    d['ctx'] = normal((BATCH, CTX_LEN, D_MODEL))
    d['c_ctx'] = normal((D_MODEL,))
    d['ada_w'] = normal((L, D_MODEL, N_MOD * D_MODEL), D_MODEL ** -0.5)
    d['ada_b'] = normal((L, N_MOD * D_MODEL), 0.02)
    d['ffn1_norm_pre'] = gain((L, D_MODEL))
    d['ffn1_norm_post'] = gain((L, D_MODEL))
    d['ffn1_w_gate'] = normal((L, D_MODEL, D_FF), D_MODEL ** -0.5)
    d['ffn1_w_up'] = normal((L, D_MODEL, D_FF), D_MODEL ** -0.5)
    d['ffn1_w_down'] = normal((L, D_FF, D_MODEL), D_FF ** -0.5)
    d['mix_norm_pre'] = gain((L, D_MODEL))
    d['mix_norm_post'] = gain((L, D_MODEL))
    d['w_in'] = normal((L, D_MODEL, PROJ_WIDTH), D_MODEL ** -0.5)
    d['fourier_w'] = normal((L, FOURIER_HEADS, FOURIER_HEAD_DIM, FOURIER_HEAD_DIM), FOURIER_HEAD_DIM ** -0.5)
    d['conv_w'] = normal((L, CONV_WIDTH, CONV_DIM), CONV_WIDTH ** -0.5)
    d['conv_b'] = normal((L, CONV_DIM), 0.02)
    dt_init = jnp.exp(jax.random.uniform(next(ks), (L, 2, SSD_HEADS), f32, math.log(1e-3), math.log(1e-1)))
    d['dt_bias'] = dt_init + jnp.log(-jnp.expm1(-dt_init))
    d['a_log'] = jnp.log(jax.random.uniform(next(ks), (L, 2, SSD_HEADS), f32, 1.0, 16.0))
    d['d_skip'] = gain((L, SSD_HEADS))
    d['ssd_norm'] = gain((L, SSD_WIDTH))
    d['w_out'] = normal((L, MIX_WIDTH, D_MODEL), MIX_WIDTH ** -0.5)
    d['ffn2_norm_pre'] = gain((L, D_MODEL))
    d['ffn2_norm_post'] = gain((L, D_MODEL))
    d['ffn2_w_gate'] = normal((L, D_MODEL, D_FF), D_MODEL ** -0.5)
    d['ffn2_w_up'] = normal((L, D_MODEL, D_FF), D_MODEL ** -0.5)
    d['ffn2_w_down'] = normal((L, D_FF, D_MODEL), D_FF ** -0.5)
    return d


def reference(x, c, ctx, c_ctx, ada_w, ada_b, ffn1_norm_pre, ffn1_norm_post, ffn1_w_gate, ffn1_w_up,
              ffn1_w_down, mix_norm_pre, mix_norm_post, w_in, fourier_w, conv_w, conv_b, dt_bias, a_log,
              d_skip, ssd_norm, w_out, ffn2_norm_pre, ffn2_norm_post, ffn2_w_gate, ffn2_w_up, ffn2_w_down):
    b = x.shape[0]
    for layer in range(DEPTH):
        mod = (jax.nn.silu(c) @ ada_w[layer] + ada_b[layer]).reshape(b, N_MOD, D_MODEL)
        mod_ctx = (jax.nn.silu(c_ctx) @ ada_w[layer] + ada_b[layer]).reshape(1, N_MOD, D_MODEL)
        ffn1 = (ffn1_norm_pre[layer], ffn1_norm_post[layer], ffn1_w_gate[layer], ffn1_w_up[layer], ffn1_w_down[layer])
        ffn2 = (ffn2_norm_pre[layer], ffn2_norm_post[layer], ffn2_w_gate[layer], ffn2_w_up[layer], ffn2_w_down[layer])
        mix = (mix_norm_pre[layer], mix_norm_post[layer], w_in[layer], fourier_w[layer], conv_w[layer],
               conv_b[layer], dt_bias[layer], a_log[layer], d_skip[layer], ssd_norm[layer], w_out[layer])

        x = macaron_ffn(x, mod[:, 0:3], *ffn1)
        ctx = macaron_ffn(ctx, mod_ctx[:, 0:3], *ffn1)

        if layer + 1 < DEPTH:
            zeros = jnp.zeros((b, SSD_HEADS, SSD_HEAD_DIM, SSD_STATE), jnp.float32)
            ctx, h_ctx_f, h_ctx_b = mix_sublayer(ctx, mod_ctx[:, 3:6], zeros, zeros, *mix)
        else:
            h_ctx_f, h_ctx_b = context_states(ctx, mod_ctx[:, 3:6], mix_norm_pre[layer], w_in[layer],
                                              conv_w[layer], conv_b[layer], dt_bias[layer], a_log[layer])

        x, _, _ = mix_sublayer(x, mod[:, 3:6], h_ctx_f, h_ctx_b, *mix)

        x = macaron_ffn(x, mod[:, 6:9], *ffn2)
        if layer + 1 < DEPTH:
            ctx = macaron_ffn(ctx, mod_ctx[:, 6:9], *ffn2)
    return x
```

```python
import functools
import math

import numpy as np
import jax
import jax.numpy as jnp
from jax import lax
from jax.experimental import pallas as pl
from jax.experimental.pallas import tpu as pltpu

F32 = jnp.float32
BF16 = jnp.bfloat16

D_MODEL = 1024
N_MOD = 9
D_FF = 2816
EPS = 1e-6
MACARON_WEIGHT = 0.5

FOURIER_WIDTH = 256
FOURIER_HEAD_DIM = 64
SSD_WIDTH = 768
SSD_HEADS = 12
SSD_HEAD_DIM = 64
SSD_GROUPS = 4
SSD_STATE = 128
GROUP_STATE = SSD_GROUPS * SSD_STATE
GROUP_WIDTH = SSD_WIDTH // SSD_GROUPS
CONV_DIM = GROUP_STATE + SSD_WIDTH + GROUP_STATE
CHUNK = 128

LANES = 128
DT_PAD = LANES
DT_BWD_OFFSET = 16
COL_Z = FOURIER_WIDTH
COL_XBC = COL_Z + SSD_WIDTH
COL_DT = COL_XBC + CONV_DIM
PROJ_PAD = COL_DT + DT_PAD

FFN_CHUNKS = ((0, 1024), (1024, 2048), (2048, 2816))
DFT_R = 64
GROUP_WINDOWS = ((0, 0, 192), (128, 64, 256), (384, 0, 192), (512, 64, 256))
VMEM_LIMIT = 56 * 1024 * 1024


def _dot(a, b):
    return jnp.dot(a, b, preferred_element_type=F32)


def _dot_nt(a, b):
    return lax.dot_general(a, b, (((1,), (1,)), ((), ())), preferred_element_type=F32)


def _dot_tn(a, b):
    return lax.dot_general(a, b, (((0,), (0,)), ((), ())), preferred_element_type=F32)


def _split3(x):
    hi = x.astype(BF16)
    r1 = x - hi.astype(F32)
    mid = r1.astype(BF16)
    lo = (r1 - mid.astype(F32)).astype(BF16)
    return hi, mid, lo


def _sigmoid(x):
    return 1.0 / (1.0 + jnp.exp(-x))


def _silu(x):
    return x * _sigmoid(x)


def _softplus(x):
    return jnp.maximum(x, 0.0) + jnp.log(1.0 + jnp.exp(-jnp.abs(x)))


def _rms(x, g):
    ms = jnp.mean(x * x, axis=-1, keepdims=True)
    return x * lax.rsqrt(ms + EPS) * g


def _mod_kernel(c_ref, w_ref, b_ref, o_ref):
    s = _silu(c_ref[...]).astype(BF16)
    o_ref[...] = _dot(s, w_ref[...].astype(BF16)) + b_ref[...]


def _mod_call(cc, ada_w, ada_b):
    rows = cc.shape[0]
    n = ada_w.shape[1]
    bn = 1024
    return pl.pallas_call(
        _mod_kernel,
        grid=(n // bn,),
        in_specs=[pl.BlockSpec((rows, D_MODEL), lambda j: (0, 0)),
                  pl.BlockSpec((D_MODEL, bn), lambda j: (0, j)),
                  pl.BlockSpec((1, bn), lambda j: (0, j))],
        out_specs=pl.BlockSpec((rows, bn), lambda j: (0, j)),
        out_shape=jax.ShapeDtypeStruct((rows, n), F32),
        compiler_params=pltpu.CompilerParams(dimension_semantics=("arbitrary",),
                                             vmem_limit_bytes=VMEM_LIMIT),
        name="mod",
    )(cc, ada_w, ada_b)


def _ffn_kernel(x_ref, mod_ref, gpre_ref, gpost_ref, wg_ref, wu_ref, wd_ref, o_ref, *, k0):
    x = x_ref[0]
    m = mod_ref[0]
    shift, scale, gate = m[k0:k0 + 1], m[k0 + 1:k0 + 2], m[k0 + 2:k0 + 3]
    h = (_rms(x, gpre_ref[...]) * (1.0 + scale) + shift).astype(BF16)
    acc = None
    for lo, hi in FFN_CHUNKS:
        g = _dot(h, wg_ref[:, lo:hi])
        u = _dot(h, wu_ref[:, lo:hi])
        a = (_silu(g) * u).astype(BF16)
        part = _dot(a, wd_ref[lo:hi, :])
        acc = part if acc is None else acc + part
    y = _rms(acc, gpost_ref[...])
    o_ref[0] = x + (MACARON_WEIGHT * gate) * y


def _resident(shape):
    nd = len(shape)
    return pl.BlockSpec(shape, lambda *_: (0,) * nd, pipeline_mode=pl.Buffered(1))


def _ffn_call(x, mod, mod_row, k0, g_pre, g_post, wg, wu, wd, tm):
    b, l, d = x.shape
    row = (lambda bi: bi) if mod_row is None else (lambda bi: mod_row)
    return pl.pallas_call(
        functools.partial(_ffn_kernel, k0=k0),
        grid=(b, l // tm),
        in_specs=[pl.BlockSpec((1, tm, d), lambda bi, i: (bi, i, 0)),
                  pl.BlockSpec((1, N_MOD, d), lambda bi, i: (row(bi), 0, 0)),
                  _resident((1, d)), _resident((1, d)),
                  _resident((d, D_FF)), _resident((d, D_FF)), _resident((D_FF, d))],
        out_specs=pl.BlockSpec((1, tm, d), lambda bi, i: (bi, i, 0)),
        out_shape=jax.ShapeDtypeStruct((b, l, d), F32),
        compiler_params=pltpu.CompilerParams(dimension_semantics=("arbitrary", "arbitrary"),
                                             vmem_limit_bytes=VMEM_LIMIT),
        name="ffn",
    )(x, mod, g_pre, g_post, wg, wu, wd)


def _inproj_kernel(x_ref, xp_ref, xn_ref, mod_ref, g_ref, w_ref, cw_ref, cb_ref, dtb_ref,
                   uf_ref, z_ref, c_ref, xs_ref, b_ref, dt_ref, *, nt, tm):
    i = pl.program_id(1)
    m = mod_ref[0]
    shift, scale = m[3:4], m[4:5]

    def hidden(v):
        return (_rms(v, g_ref[...]) * (1.0 + scale) + shift).astype(BF16)

    p = _dot(hidden(x_ref[0]), w_ref[...])
    halo = jnp.concatenate([xp_ref[0], xn_ref[0]], axis=0)
    ph = _dot(hidden(halo), w_ref[:, COL_XBC:COL_DT])
    prev = jnp.where(i > 0, ph[7:8], 0.0)
    nxt = jnp.where(i < nt - 1, ph[8:9], 0.0)
    u = p[:, COL_XBC:COL_DT]
    rows = lax.broadcasted_iota(jnp.int32, u.shape, 0)
    um1 = jnp.where(rows == 0, prev, pltpu.roll(u, 1, 0))
    up1 = jnp.where(rows == tm - 1, nxt, pltpu.roll(u, tm - 1, 0))
    cw = cw_ref[...]
    v = _silu(um1 * cw[0:1] + u * cw[1:2] + up1 * cw[2:3] + cb_ref[...])
    c_ref[0] = v[:, :GROUP_STATE].astype(BF16)
    xs_ref[0] = v[:, GROUP_STATE:GROUP_STATE + SSD_WIDTH]
    b_ref[0] = v[:, GROUP_STATE + SSD_WIDTH:].astype(BF16)
    for hf in range(FOURIER_WIDTH // LANES):
        uf_ref[0, hf] = p[:, hf * LANES:(hf + 1) * LANES]
    z_ref[0] = p[:, COL_Z:COL_XBC]
    dt_ref[0] = _softplus(p[:, COL_DT:] + dtb_ref[...])


def _inproj_call(x, mod, mod_row, g_pre, w_pad, conv_w, conv_b, dtb_pad, tm):
    b, l, d = x.shape
    nt = l // tm
    hb = tm // 8
    row = (lambda bi: bi) if mod_row is None else (lambda bi: mod_row)
    shapes = [(SSD_WIDTH, F32), (GROUP_STATE, BF16), (SSD_WIDTH, F32), (GROUP_STATE, BF16), (DT_PAD, F32)]
    nh = FOURIER_WIDTH // LANES
    uf_spec = pl.BlockSpec((1, nh, tm, LANES), lambda bi, i: (bi, 0, i, 0))
    uf_shape = jax.ShapeDtypeStruct((b, nh, l, LANES), F32)
    return pl.pallas_call(
        functools.partial(_inproj_kernel, nt=nt, tm=tm),
        grid=(b, nt),
        in_specs=[pl.BlockSpec((1, tm, d), lambda bi, i: (bi, i, 0)),
                  pl.BlockSpec((1, 8, d), lambda bi, i: (bi, jnp.maximum(i * hb - 1, 0), 0)),
                  pl.BlockSpec((1, 8, d), lambda bi, i: (bi, jnp.minimum((i + 1) * hb, l // 8 - 1), 0)),
                  pl.BlockSpec((1, N_MOD, d), lambda bi, i: (row(bi), 0, 0)),
                  _resident((1, d)), _resident((d, PROJ_PAD)), _resident((3, CONV_DIM)),
                  _resident((1, CONV_DIM)), _resident((1, DT_PAD))],
        out_specs=[uf_spec] + [pl.BlockSpec((1, tm, w), lambda bi, i: (bi, i, 0)) for w, _ in shapes],
        out_shape=[uf_shape] + [jax.ShapeDtypeStruct((b, l, w), dt) for w, dt in shapes],
        compiler_params=pltpu.CompilerParams(dimension_semantics=("arbitrary", "arbitrary"),
                                             vmem_limit_bytes=VMEM_LIMIT),
        name="inproj",
    )(x, x, x, mod, g_pre, w_pad, conv_w, conv_b, dtb_pad)


def _ssd_direction(c, bm, xs, dt, a_row, expand, tri, h_ref, d, *, backward, emit_y):
    q = CHUNK
    off = DT_BWD_OFFSET if backward else 0
    da = dt * a_row
    hi, mid, lo = _split3(da)
    cs = _dot(tri, hi) + _dot(tri, mid) + _dot(tri, lo)
    end = 0 if backward else q - 1
    cs_end = cs[end:end + 1, :]
    dec = jnp.exp(cs_end - cs) * dt
    e_end = jnp.broadcast_to(jnp.exp(cs_end), (8, LANES))
    pieces = [dec, jnp.exp(cs), dt, e_end] if emit_y else [dec, e_end]
    big = jnp.concatenate(pieces, axis=0)
    hi, mid, lo = _split3(big)
    ex = _dot(hi, expand) + _dot(mid, expand) + _dot(lo, expand)
    chunk_decay = ex[big.shape[0] - 8:big.shape[0] - 7]
    xdec = (xs * ex[:q]).astype(BF16)
    lane = lax.broadcasted_iota(jnp.int32, (1, 2 * LANES), 1)

    y = None
    if emit_y:
        ecs_e = ex[q:2 * q]
        xdt = (xs * ex[2 * q:3 * q]).astype(BF16)
        cs_t = cs.T
        ri = lax.broadcasted_iota(jnp.int32, (q, q), 0)
        ci = lax.broadcasted_iota(jnp.int32, (q, q), 1)
        keep = (ri <= ci) if backward else (ri >= ci)
        cbs = [_dot_nt(c[:, g * SSD_STATE:(g + 1) * SSD_STATE], bm[:, g * SSD_STATE:(g + 1) * SSD_STATE])
               for g in range(SSD_GROUPS)]
        lane1 = lax.broadcasted_iota(jnp.int32, (1, LANES), 1)
        ytiles = []
        for j in range(SSD_HEADS // 2):
            lms, xparts = [], []
            xpair = xdt[:, j * LANES:(j + 1) * LANES]
            for t in range(2):
                h = 2 * j + t
                col = cs[:, off + h:off + h + 1]
                row = cs_t[off + h:off + h + 1, :]
                seg = jnp.exp(jnp.where(keep, col - row, -jnp.inf))
                lms.append((cbs[h // 3] * seg).astype(BF16))
                half = (lane1 < SSD_HEAD_DIM) if t == 0 else (lane1 >= SSD_HEAD_DIM)
                xparts.append(jnp.where(half, xpair, jnp.zeros_like(xpair)))
            ytiles.append(_dot(jnp.concatenate(lms, axis=1), jnp.concatenate(xparts, axis=0)))
        y_diag = jnp.concatenate(ytiles, axis=1)

    yo = []
    for g, (start, lo_l, hi_l) in enumerate(GROUP_WINDOWS):
        valid = (lane >= lo_l) & (lane < hi_l)
        hprev = h_ref[d, g]
        if emit_y:
            yo.append(_dot(c[:, g * SSD_STATE:(g + 1) * SSD_STATE], hprev.astype(BF16)))
        xw = xdec[:, start:start + 2 * LANES]
        xw = jnp.where(valid, xw, jnp.zeros_like(xw))
        s_new = _dot_tn(bm[:, g * SSD_STATE:(g + 1) * SSD_STATE], xw)
        h_ref[d, g] = hprev * chunk_decay[:, start:start + 2 * LANES] + s_new
    if emit_y:
        y_off = jnp.concatenate([yo[0][:, :LANES], yo[0][:, LANES:] + yo[1][:, :LANES], yo[1][:, LANES:],
                                 yo[2][:, :LANES], yo[2][:, LANES:] + yo[3][:, :LANES], yo[3][:, LANES:]],
                                axis=1)
        y = y_diag + y_off * ecs_e
    return y


def _ssd_kernel(cf_ref, bf_ref, xf_ref, dtf_ref, cb_ref, bb_ref, xb_ref, dtb_ref, h0_ref, alog_ref,
                valid_ref, expand_ref, *rest, nc, emit_y, emit_state):
    outs = list(rest)
    h_ref = outs.pop()
    i = pl.program_id(1)

    @pl.when(i == 0)
    def _():
        h_ref[...] = h0_ref[0]

    a = -jnp.exp(alog_ref[...]) * valid_ref[...]
    ri = lax.broadcasted_iota(jnp.int32, (CHUNK, CHUNK), 0)
    ci = lax.broadcasted_iota(jnp.int32, (CHUNK, CHUNK), 1)
    tril = jnp.where(ri >= ci, 1.0, 0.0).astype(BF16)
    triu = jnp.where(ri <= ci, 1.0, 0.0).astype(BF16)
    ef = expand_ref[0].astype(BF16)
    eb = expand_ref[1].astype(BF16)
    yf = _ssd_direction(cf_ref[0], bf_ref[0], xf_ref[0], dtf_ref[0], a[0:1], ef, tril, h_ref, 0,
                        backward=False, emit_y=emit_y)
    yb = _ssd_direction(cb_ref[0], bb_ref[0], xb_ref[0], dtb_ref[0], a[1:2], eb, triu, h_ref, 1,
                        backward=True, emit_y=emit_y)
    if emit_y:
        outs[0][0] = yf
        outs[1][0] = yb
    if emit_state:
        @pl.when(i == nc - 1)
        def _():
            outs[-1][0] = h_ref[...]


def _ssd_call(cm, bm, xs, dt, h0, alog_pad, valid, expand, *, emit_y, emit_state):
    b, l, _ = xs.shape
    nc = l // CHUNK
    q = CHUNK

    def fwd(w):
        return pl.BlockSpec((1, q, w), lambda bi, i: (bi, i, 0))

    def bwd(w):
        return pl.BlockSpec((1, q, w), lambda bi, i: (bi, nc - 1 - i, 0))

    state_shape = (2, SSD_GROUPS, SSD_STATE, 2 * LANES)
    state_spec = pl.BlockSpec((1,) + state_shape, lambda bi, i: (bi, 0, 0, 0, 0))
    out_specs, out_shape = [], []
    if emit_y:
        out_specs += [fwd(SSD_WIDTH), bwd(SSD_WIDTH)]
        out_shape += [jax.ShapeDtypeStruct((b, l, SSD_WIDTH), F32)] * 2
    if emit_state:
        out_specs.append(state_spec)
        out_shape.append(jax.ShapeDtypeStruct((b,) + state_shape, F32))
    return pl.pallas_call(
        functools.partial(_ssd_kernel, nc=nc, emit_y=emit_y, emit_state=emit_state),
        grid=(b, nc),
        in_specs=[fwd(GROUP_STATE), fwd(GROUP_STATE), fwd(SSD_WIDTH), fwd(DT_PAD),
                  bwd(GROUP_STATE), bwd(GROUP_STATE), bwd(SSD_WIDTH), bwd(DT_PAD),
                  state_spec, _resident((2, DT_PAD)), _resident((2, DT_PAD)),
                  _resident((2, DT_PAD, SSD_WIDTH))],
        out_specs=out_specs,
        out_shape=out_shape,
        scratch_shapes=[pltpu.VMEM(state_shape, F32)],
        compiler_params=pltpu.CompilerParams(dimension_semantics=("arbitrary", "arbitrary"),
                                             vmem_limit_bytes=VMEM_LIMIT),
        name="ssd_y" if emit_y else "ssd_state",
    )(cm, bm, xs, dt, cm, bm, xs, dt, h0, alog_pad, valid, expand)


def _fourier_kernel(u_ref, g_ref, h_ref, m1_ref, w_ref, o_ref, y_scr, z_scr):
    r = DFT_R
    for l2 in range(r):
        slab = jnp.concatenate([u_ref[0, hf, pl.ds(l2, r, stride=r), :] for hf in range(2)], axis=1)
        y = _dot(g_ref[l2].astype(BF16), slab.astype(BF16))
        for hf in range(2):
            y_scr[hf, l2 * 2 * r:(l2 + 1) * 2 * r, :] = y[:, hf * LANES:(hf + 1) * LANES]
    hm = h_ref[...].astype(BF16)
    for k1 in range(r):
        parts = [jnp.concatenate([y_scr[hf, pl.ds(base + k1, r, stride=2 * r), :] for hf in range(2)], axis=1)
                 for base in (0, r)]
        zz = _dot(hm, jnp.concatenate(parts, axis=0).astype(BF16))
        for part in range(2):
            for hf in range(2):
                z_scr[2 * part + hf, pl.ds(k1, r, stride=r), :] = (
                    zz[part * r:(part + 1) * r, hf * LANES:(hf + 1) * LANES])
    zc = jnp.concatenate([z_scr[t] for t in range(4)], axis=1)
    f = _dot(zc.astype(BF16), m1_ref[...].astype(BF16))
    o_ref[0] = _dot(f.astype(BF16), w_ref[...])


def _fourier_call(uf, g_tab, h_tab, m1_tab, w_bd):
    b, _, l, _ = uf.shape
    w = FOURIER_WIDTH
    r = DFT_R
    return pl.pallas_call(
        _fourier_kernel,
        grid=(b,),
        in_specs=[pl.BlockSpec((1, 2, l, LANES), lambda bi: (bi, 0, 0, 0)),
                  _resident((r, 2 * r, r)), _resident((2 * r, 2 * r)), _resident((2 * w, w)),
                  _resident((w, w))],
        out_specs=pl.BlockSpec((1, l, w), lambda bi: (bi, 0, 0)),
        out_shape=jax.ShapeDtypeStruct((b, l, w), F32),
        scratch_shapes=[pltpu.VMEM((2, 2 * l, LANES), F32), pltpu.VMEM((4, l, LANES), F32)],
        compiler_params=pltpu.CompilerParams(dimension_semantics=("arbitrary",),
                                             vmem_limit_bytes=VMEM_LIMIT),
        name="fourier",
    )(uf, g_tab, h_tab, m1_tab, w_bd)


def _dft_tables():
    r = DFT_R
    n = r * r
    l2 = np.arange(r)[:, None, None]
    k1 = np.arange(r)[None, :, None]
    l1 = np.arange(r)[None, None, :]
    ang = 2.0 * np.pi * ((k1 * (r * l1 + l2)) % n) / n
    g_tab = np.concatenate([np.cos(ang), -np.sin(ang)], axis=1)
    kk = np.arange(r)[:, None]
    ll = np.arange(r)[None, :]
    ang2 = 2.0 * np.pi * ((kk * ll) % r) / r
    c2, s2 = np.cos(ang2), np.sin(ang2)
    h_tab = np.block([[c2, s2], [-s2, c2]])
    heads = FOURIER_WIDTH // FOURIER_HEAD_DIM
    bd = lambda m: np.kron(np.eye(heads), m)
    scale = 1.0 / math.sqrt(n * FOURIER_HEAD_DIM)
    m1 = np.concatenate([bd(c2), bd(s2)], axis=0) * scale
    return (jnp.asarray(g_tab, F32), jnp.asarray(h_tab, F32), jnp.asarray(m1, F32))


def _mixout_kernel(yf_ref, yb_ref, xs_ref, z_ref, fo_ref, x_ref, mod_ref, dskip_ref, gssd_ref, gpost_ref,
                   wf_ref, wy_ref, o_ref):
    m = mod_ref[0]
    gate = m[5:6]
    y = yf_ref[0] + yb_ref[0] + dskip_ref[...] * xs_ref[0]
    y = y * _silu(z_ref[0])
    lane = lax.broadcasted_iota(jnp.int32, (1, SSD_WIDTH), 1)
    y2 = y * y
    rs = jnp.zeros_like(y)
    for g in range(SSD_GROUPS):
        mg = (lane >= g * GROUP_WIDTH) & (lane < (g + 1) * GROUP_WIDTH)
        ms = jnp.sum(jnp.where(mg, y2, 0.0), axis=-1, keepdims=True) * (1.0 / GROUP_WIDTH)
        rs = jnp.where(mg, lax.rsqrt(ms + EPS), rs)
    yn = (y * rs * gssd_ref[...]).astype(BF16)
    out = _dot(fo_ref[0].astype(BF16), wf_ref[...]) + _dot(yn, wy_ref[...])
    o_ref[0] = x_ref[0] + gate * _rms(out, gpost_ref[...])


def _mixout_call(yf, yb, xs, z, fo, x, mod, dskip, g_ssd, g_post, w_f, w_y, tm):
    b, l, d = x.shape

    def tok(w):
        return pl.BlockSpec((1, tm, w), lambda bi, i: (bi, i, 0))

    return pl.pallas_call(
        _mixout_kernel,
        grid=(b, l // tm),
        in_specs=[tok(SSD_WIDTH), tok(SSD_WIDTH), tok(SSD_WIDTH), tok(SSD_WIDTH), tok(FOURIER_WIDTH), tok(d),
                  pl.BlockSpec((1, N_MOD, d), lambda bi, i: (bi, 0, 0)),
                  _resident((1, SSD_WIDTH)), _resident((1, SSD_WIDTH)), _resident((1, d)),
                  _resident((FOURIER_WIDTH, d)), _resident((SSD_WIDTH, d))],
        out_specs=tok(d),
        out_shape=jax.ShapeDtypeStruct((b, l, d), F32),
        compiler_params=pltpu.CompilerParams(dimension_semantics=("arbitrary", "arbitrary"),
                                             vmem_limit_bytes=VMEM_LIMIT),
        name="mixout",
    )(yf, yb, xs, z, fo, x, mod, dskip, g_ssd, g_post, w_f, w_y)


def _pad_heads(v):
    out = jnp.zeros((2, DT_PAD), F32)
    out = out.at[0, :SSD_HEADS].set(v[0])
    return out.at[1, DT_BWD_OFFSET:DT_BWD_OFFSET + SSD_HEADS].set(v[1])


def _expand_tables():
    e = np.zeros((2, DT_PAD, SSD_WIDTH), np.float32)
    for h in range(SSD_HEADS):
        e[0, h, h * SSD_HEAD_DIM:(h + 1) * SSD_HEAD_DIM] = 1.0
        e[1, DT_BWD_OFFSET + h, h * SSD_HEAD_DIM:(h + 1) * SSD_HEAD_DIM] = 1.0
    v = np.zeros((2, DT_PAD), np.float32)
    v[0, :SSD_HEADS] = 1.0
    v[1, DT_BWD_OFFSET:DT_BWD_OFFSET + SSD_HEADS] = 1.0
    return jnp.asarray(e), jnp.asarray(v)


def kernel(x, c, ctx, c_ctx, ada_w, ada_b, ffn1_norm_pre, ffn1_norm_post, ffn1_w_gate, ffn1_w_up, ffn1_w_down, mix_norm_pre, mix_norm_post, w_in, fourier_w, conv_w, conv_b, dt_bias, a_log, d_skip, ssd_norm, w_out, ffn2_norm_pre, ffn2_norm_post, ffn2_w_gate, ffn2_w_up, ffn2_w_down):
    b, l, d = x.shape
    lyr = 0
    ctx_row = b

    cc = jnp.zeros((16, d), F32).at[:b].set(c).at[b].set(c_ctx)
    mod = _mod_call(cc, ada_w[lyr], ada_b[lyr][None, :]).reshape(16, N_MOD, d)

    bf = lambda w: w.astype(BF16)
    row = lambda v: v[None, :]
    f1 = (row(ffn1_norm_pre[lyr]), row(ffn1_norm_post[lyr]), bf(ffn1_w_gate[lyr]), bf(ffn1_w_up[lyr]),
          bf(ffn1_w_down[lyr]))
    f2 = (row(ffn2_norm_pre[lyr]), row(ffn2_norm_post[lyr]), bf(ffn2_w_gate[lyr]), bf(ffn2_w_up[lyr]),
          bf(ffn2_w_down[lyr]))

    wi = w_in[lyr]
    n_main = COL_DT
    w_dt = jnp.zeros((d, DT_PAD), F32)
    w_dt = w_dt.at[:, :SSD_HEADS].set(wi[:, n_main:n_main + SSD_HEADS])
    w_dt = w_dt.at[:, DT_BWD_OFFSET:DT_BWD_OFFSET + SSD_HEADS].set(wi[:, n_main + SSD_HEADS:])
    w_pad = bf(jnp.concatenate([wi[:, :n_main], w_dt], axis=1))
    dtb_pad = _pad_heads(dt_bias[lyr])
    dtb_f = dtb_pad[0:1] + dtb_pad[1:2]
    alog_pad = _pad_heads(a_log[lyr])
    expand, valid = _expand_tables()
    conv_args = (conv_w[lyr], row(conv_b[lyr]), dtb_f)

    x1 = _ffn_call(x, mod, None, 0, *f1, tm=512)
    ctx1 = _ffn_call(ctx, mod, ctx_row, 0, *f1, tm=ctx.shape[1])

    zeros_state = jnp.zeros((b, 2, SSD_GROUPS, SSD_STATE, 2 * LANES), F32)
    _, _, _, xs_c, bm_c, dt_c = _inproj_call(ctx1, mod, ctx_row, row(mix_norm_pre[lyr]), w_pad, *conv_args,
                                             tm=ctx.shape[1])
    cm_dummy = bm_c
    (h_ctx,) = _ssd_call(cm_dummy, bm_c, xs_c, dt_c, zeros_state, alog_pad, valid, expand,
                         emit_y=False, emit_state=True)

    uf, z, cm, xs, bm, dt = _inproj_call(x1, mod, None, row(mix_norm_pre[lyr]), w_pad, *conv_args, tm=512)
    yf, yb = _ssd_call(cm, bm, xs, dt, h_ctx, alog_pad, valid, expand, emit_y=True, emit_state=False)
    g_tab, h_tab, m1_tab = _dft_tables()
    heads = FOURIER_WIDTH // FOURIER_HEAD_DIM
    w_bd = jnp.zeros((FOURIER_WIDTH, FOURIER_WIDTH), F32)
    for hd in range(heads):
        s = slice(hd * FOURIER_HEAD_DIM, (hd + 1) * FOURIER_HEAD_DIM)
        w_bd = w_bd.at[s, s].set(fourier_w[lyr, hd])
    fo = _fourier_call(uf, g_tab, h_tab, m1_tab, bf(w_bd))
    dskip = jnp.repeat(d_skip[lyr], SSD_HEAD_DIM)[None, :]
    wo = bf(w_out[lyr])
    x2 = _mixout_call(yf, yb, xs, z, fo, x1, mod, dskip, row(ssd_norm[lyr]), row(mix_norm_post[lyr]),
                      wo[:FOURIER_WIDTH], wo[FOURIER_WIDTH:], tm=512)

    return _ffn_call(x2, mod, None, 6, *f2, tm=512)
```

```python
import functools
import math

import numpy as np
import jax
import jax.numpy as jnp
from jax import lax
from jax.experimental import pallas as pl
from jax.experimental.pallas import tpu as pltpu

F32 = jnp.float32
BF16 = jnp.bfloat16

D_MODEL = 1024
N_MOD = 9
D_FF = 2816
EPS = 1e-6
MACARON_WEIGHT = 0.5

FOURIER_WIDTH = 256
FOURIER_HEAD_DIM = 64
SSD_WIDTH = 768
SSD_HEADS = 12
SSD_HEAD_DIM = 64
SSD_GROUPS = 4
SSD_STATE = 128
GROUP_STATE = SSD_GROUPS * SSD_STATE
GROUP_WIDTH = SSD_WIDTH // SSD_GROUPS
CONV_DIM = GROUP_STATE + SSD_WIDTH + GROUP_STATE
CHUNK = 128

LANES = 128
DT_PAD = LANES
DT_BWD_OFFSET = 16
HEAD_ROWS = 16
SSD_BLOCK_CHUNKS = 4
COL_Z = FOURIER_WIDTH
COL_XBC = COL_Z + SSD_WIDTH
COL_DT = COL_XBC + CONV_DIM
PROJ_PAD = COL_DT + DT_PAD

FFN_CHUNKS = ((0, 1024), (1024, 2048), (2048, 2816))
DFT_R = 64
OFF_TILE_SPLITS = (192, 128, 64)
LOG2E = 1.4426950408889634
VMEM_LIMIT = 56 * 1024 * 1024


def _dot(a, b):
    return jnp.dot(a, b, preferred_element_type=F32)


def _dot_nt(a, b):
    return lax.dot_general(a, b, (((1,), (1,)), ((), ())), preferred_element_type=F32)


def _dot_tn(a, b):
    return lax.dot_general(a, b, (((0,), (0,)), ((), ())), preferred_element_type=F32)


def _split3(x):
    hi = x.astype(BF16)
    r1 = x - hi.astype(F32)
    mid = r1.astype(BF16)
    lo = (r1 - mid.astype(F32)).astype(BF16)
    return hi, mid, lo


def _sigmoid(x):
    return 1.0 / (1.0 + jnp.exp(-x))


def _silu(x):
    return x * _sigmoid(x)


def _softplus(x):
    return jnp.maximum(x, 0.0) + jnp.log(1.0 + jnp.exp(-jnp.abs(x)))


def _rms(x, g):
    ms = jnp.mean(x * x, axis=-1, keepdims=True)
    return x * lax.rsqrt(ms + EPS) * g


def _mod_kernel(c_ref, w_ref, b_ref, o_ref):
    s = _silu(c_ref[...]).astype(BF16)
    o_ref[...] = _dot(s, w_ref[...].astype(BF16)) + b_ref[...]


def _mod_call(cc, ada_w, ada_b):
    rows = cc.shape[0]
    n = ada_w.shape[1]
    bn = 1024
    return pl.pallas_call(
        _mod_kernel,
        grid=(n // bn,),
        in_specs=[pl.BlockSpec((rows, D_MODEL), lambda j: (0, 0)),
                  pl.BlockSpec((D_MODEL, bn), lambda j: (0, j)),
                  pl.BlockSpec((1, bn), lambda j: (0, j))],
        out_specs=pl.BlockSpec((rows, bn), lambda j: (0, j)),
        out_shape=jax.ShapeDtypeStruct((rows, n), F32),
        compiler_params=pltpu.CompilerParams(dimension_semantics=("arbitrary",),
                                             vmem_limit_bytes=VMEM_LIMIT),
        name="mod",
    )(cc, ada_w, ada_b)


def _ffn_kernel(x_ref, mod_ref, gpre_ref, gpost_ref, wg_ref, wu_ref, wd_ref, o_ref, *, k0):
    x = x_ref[0]
    m = mod_ref[0]
    shift, scale, gate = m[k0:k0 + 1], m[k0 + 1:k0 + 2], m[k0 + 2:k0 + 3]
    h = (_rms(x, gpre_ref[...]) * (1.0 + scale) + shift).astype(BF16)
    acc = None
    for lo, hi in FFN_CHUNKS:
        g = _dot(h, wg_ref[:, lo:hi])
        u = _dot(h, wu_ref[:, lo:hi])
        a = (_silu(g) * u).astype(BF16)
        part = _dot(a, wd_ref[lo:hi, :])
        acc = part if acc is None else acc + part
    y = _rms(acc, gpost_ref[...])
    o_ref[0] = x + (MACARON_WEIGHT * gate) * y


def _resident(shape):
    nd = len(shape)
    return pl.BlockSpec(shape, lambda *_: (0,) * nd, pipeline_mode=pl.Buffered(1))


def _ffn_call(x, mod, mod_row, k0, g_pre, g_post, wg, wu, wd, tm):
    b, l, d = x.shape
    row = (lambda bi: bi) if mod_row is None else (lambda bi: mod_row)
    return pl.pallas_call(
        functools.partial(_ffn_kernel, k0=k0),
        grid=(b, l // tm),
        in_specs=[pl.BlockSpec((1, tm, d), lambda bi, i: (bi, i, 0)),
                  pl.BlockSpec((1, N_MOD, d), lambda bi, i: (row(bi), 0, 0)),
                  _resident((1, d)), _resident((1, d)),
                  _resident((d, D_FF)), _resident((d, D_FF)), _resident((D_FF, d))],
        out_specs=pl.BlockSpec((1, tm, d), lambda bi, i: (bi, i, 0)),
        out_shape=jax.ShapeDtypeStruct((b, l, d), F32),
        compiler_params=pltpu.CompilerParams(dimension_semantics=("arbitrary", "arbitrary"),
                                             vmem_limit_bytes=VMEM_LIMIT),
        name="ffn",
    )(x, mod, g_pre, g_post, wg, wu, wd)


def _inproj_kernel(x_ref, xp_ref, xn_ref, mod_ref, g_ref, w_ref, cw_ref, cb_ref, dtb_ref,
                   uf_ref, z_ref, c_ref, xs_ref, b_ref, dt_ref, *, nt, tm):
    i = pl.program_id(1)
    m = mod_ref[0]
    shift, scale = m[3:4], m[4:5]

    def hidden(v):
        return (_rms(v, g_ref[...]) * (1.0 + scale) + shift).astype(BF16)

    p = _dot(hidden(x_ref[0]), w_ref[...])
    halo = jnp.concatenate([xp_ref[0], xn_ref[0]], axis=0)
    ph = _dot(hidden(halo), w_ref[:, COL_XBC:COL_DT])
    prev = jnp.where(i > 0, ph[7:8], 0.0)
    nxt = jnp.where(i < nt - 1, ph[8:9], 0.0)
    u = p[:, COL_XBC:COL_DT]
    rows = lax.broadcasted_iota(jnp.int32, u.shape, 0)
    um1 = jnp.where(rows == 0, prev, pltpu.roll(u, 1, 0))
    up1 = jnp.where(rows == tm - 1, nxt, pltpu.roll(u, tm - 1, 0))
    cw = cw_ref[...]
    v = _silu(um1 * cw[0:1] + u * cw[1:2] + up1 * cw[2:3] + cb_ref[...])
    c_ref[0] = v[:, :GROUP_STATE].astype(BF16)
    xs_ref[0] = v[:, GROUP_STATE:GROUP_STATE + SSD_WIDTH].astype(BF16)
    b_ref[0] = v[:, GROUP_STATE + SSD_WIDTH:].astype(BF16)
    for hf in range(FOURIER_WIDTH // LANES):
        uf_ref[0, hf] = p[:, hf * LANES:(hf + 1) * LANES]
    z_ref[0] = p[:, COL_Z:COL_XBC].astype(BF16)
    dt_ref[0] = _softplus(p[:, COL_DT:] + dtb_ref[...])


def _inproj_call(x, mod, mod_row, g_pre, w_pad, conv_w, conv_b, dtb_pad, tm):
    b, l, d = x.shape
    nt = l // tm
    hb = tm // 8
    row = (lambda bi: bi) if mod_row is None else (lambda bi: mod_row)
    shapes = [(SSD_WIDTH, BF16), (GROUP_STATE, BF16), (SSD_WIDTH, BF16), (GROUP_STATE, BF16), (DT_PAD, F32)]
    nh = FOURIER_WIDTH // LANES
    uf_spec = pl.BlockSpec((1, nh, tm, LANES), lambda bi, i: (bi, 0, i, 0))
    uf_shape = jax.ShapeDtypeStruct((b, nh, l, LANES), F32)
    return pl.pallas_call(
        functools.partial(_inproj_kernel, nt=nt, tm=tm),
        grid=(b, nt),
        in_specs=[pl.BlockSpec((1, tm, d), lambda bi, i: (bi, i, 0)),
                  pl.BlockSpec((1, 8, d), lambda bi, i: (bi, jnp.maximum(i * hb - 1, 0), 0)),
                  pl.BlockSpec((1, 8, d), lambda bi, i: (bi, jnp.minimum((i + 1) * hb, l // 8 - 1), 0)),
                  pl.BlockSpec((1, N_MOD, d), lambda bi, i: (row(bi), 0, 0)),
                  _resident((1, d)), _resident((d, PROJ_PAD)), _resident((3, CONV_DIM)),
                  _resident((1, CONV_DIM)), _resident((1, DT_PAD))],
        out_specs=[uf_spec] + [pl.BlockSpec((1, tm, w), lambda bi, i: (bi, i, 0)) for w, _ in shapes],
        out_shape=[uf_shape] + [jax.ShapeDtypeStruct((b, l, w), dt) for w, dt in shapes],
        compiler_params=pltpu.CompilerParams(dimension_semantics=("arbitrary", "arbitrary"),
                                             vmem_limit_bytes=VMEM_LIMIT),
        name="inproj",
    )(x, x, x, mod, g_pre, w_pad, conv_w, conv_b, dtb_pad)


def _ssd_chunk(c, bm, xs, dt_t, cs_t, keep, h, *, backward, emit_y):
    q = CHUNK
    end = 0 if backward else q - 1
    cs2_t = cs_t * LOG2E
    row_t = cs2_t - jnp.log2(dt_t)
    dec_t = (jnp.exp2(cs2_t[:, end:end + 1] - cs2_t) * dt_t).astype(BF16)
    cs2 = jnp.concatenate([cs2_t, jnp.zeros((q - HEAD_ROWS, q), F32)], axis=0).T
    e_end = jnp.broadcast_to(jnp.exp2(cs2[end:end + 1, :]), (8, LANES))

    groups = [slice(g * SSD_STATE, (g + 1) * SSD_STATE) for g in range(SSD_GROUPS)]
    b_t = [bm[:, s].T for s in groups]
    lane1 = lax.broadcasted_iota(jnp.int32, (1, LANES), 1)
    lane2 = lax.broadcasted_iota(jnp.int32, (1, 2 * LANES), 1)
    first = lane1 < SSD_HEAD_DIM
    if emit_y:
        zt = jnp.zeros((SSD_STATE, q), BF16)
        cbs = []
        for g0 in range(0, SSD_GROUPS, 2):
            b_bd = jnp.concatenate([jnp.concatenate([b_t[g0], zt], axis=1),
                                    jnp.concatenate([zt, b_t[g0 + 1]], axis=1)], axis=0)
            cb2 = _dot(c[:, groups[g0].start:groups[g0 + 1].stop], b_bd).astype(BF16)
            cbs += [cb2[:, :q], cb2[:, q:]]
        hb = h.astype(BF16)
        yo = []
        for n, split in enumerate(OFF_TILE_SPLITS):
            hw = hb[:, n * 2 * LANES:(n + 1) * 2 * LANES]
            zero = jnp.zeros_like(hw)
            rhs = jnp.concatenate([jnp.where(lane2 < split, hw, zero), jnp.where(lane2 >= split, hw, zero)],
                                  axis=0)
            yo.append(_dot(c[:, n * SSD_STATE:(n + 2) * SSD_STATE], rhs))
    y_tiles, s_tiles, d_tiles = [], [], []
    for j in range(SSD_HEADS // 2):
        xpair = xs[:, j * LANES:(j + 1) * LANES]
        zero = jnp.zeros_like(xpair)
        rhs = jnp.concatenate([jnp.where(first, xpair, zero), jnp.where(first, zero, xpair)], axis=0)
        w_parts, l_parts, e_parts, d_parts = [], [], [], []
        for t in range(2):
            hd = 2 * j + t
            g = hd // (SSD_HEADS // SSD_GROUPS)
            w_parts.append(b_t[g] * dec_t[hd:hd + 1, :])
            d_parts.append(jnp.broadcast_to(e_end[:, hd:hd + 1], (8, LANES)))
            if emit_y:
                col = jnp.broadcast_to(cs2[:, hd:hd + 1], (q, q))
                seg = jnp.exp2(jnp.where(keep, col - row_t[hd:hd + 1, :], -jnp.inf))
                l_parts.append(cbs[g] * seg.astype(BF16))
                e_parts.append(jnp.exp2(col))
        d_tiles.append(jnp.where(first, d_parts[0], d_parts[1])[0:1])
        lhs = jnp.concatenate(w_parts, axis=1)
        if emit_y:
            lhs = jnp.concatenate([jnp.concatenate(l_parts, axis=1), lhs], axis=0)
        out = _dot(lhs, rhs)
        if emit_y:
            y_off = yo[j // 2][:, (j % 2) * LANES:(j % 2 + 1) * LANES]
            y_tiles.append(out[:q] + y_off * jnp.where(first, e_parts[0], e_parts[1]))
            s_tiles.append(out[q:])
        else:
            s_tiles.append(out)
    s_new = jnp.concatenate(s_tiles, axis=1)
    chunk_decay = jnp.concatenate(d_tiles, axis=1)
    y = jnp.concatenate(y_tiles, axis=1) if emit_y else None
    return y, h * chunk_decay + s_new


def _block_cumsum(dt, a_t, tri_t, off, nsub):
    dt_all = dt.T[off:off + HEAD_ROWS]
    dts = [dt_all[:, t * CHUNK:(t + 1) * CHUNK] for t in range(nsub)]
    splits = [_split3(d * a_t) for d in dts]
    lhs = jnp.concatenate([s[k] for k in range(3) for s in splits], axis=0)
    r = _dot(lhs, tri_t)
    rows = lambda k, t: r[(k * nsub + t) * HEAD_ROWS:(k * nsub + t + 1) * HEAD_ROWS]
    return dts, [rows(0, t) + rows(1, t) + rows(2, t) for t in range(nsub)]


def _ssd_kernel(cf_ref, bf_ref, xf_ref, dtf_ref, cb_ref, bb_ref, xb_ref, dtb_ref, h0_ref, alog_ref,
                valid_ref, *rest, nb, nsub, emit_y, emit_state):
    outs = list(rest)
    h_ref = outs.pop()
    i = pl.program_id(1)

    @pl.when(i == 0)
    def _():
        h_ref[...] = h0_ref[0]

    a_t = -jnp.exp(alog_ref[...]) * valid_ref[...]
    ri = lax.broadcasted_iota(jnp.int32, (CHUNK, CHUNK), 0)
    ci = lax.broadcasted_iota(jnp.int32, (CHUNK, CHUNK), 1)
    lower, upper = ri >= ci, ri <= ci
    tril = jnp.where(lower, 1.0, 0.0).astype(BF16)
    triu = jnp.where(upper, 1.0, 0.0).astype(BF16)
    dtf, csf = _block_cumsum(dtf_ref[0], a_t[0], triu, 0, nsub)
    dtb, csb = _block_cumsum(dtb_ref[0], a_t[1], tril, DT_BWD_OFFSET, nsub)
    hf, hb = h_ref[0], h_ref[1]
    for t in range(nsub):
        tb = nsub - 1 - t
        sf = slice(t * CHUNK, (t + 1) * CHUNK)
        sb = slice(tb * CHUNK, (tb + 1) * CHUNK)
        yf, hf = _ssd_chunk(cf_ref[0, sf], bf_ref[0, sf], xf_ref[0, sf], dtf[t], csf[t], lower, hf,
                            backward=False, emit_y=emit_y)
        yb, hb = _ssd_chunk(cb_ref[0, sb], bb_ref[0, sb], xb_ref[0, sb], dtb[tb], csb[tb], upper, hb,
                            backward=True, emit_y=emit_y)
        if emit_y:
            outs[0][0, sf] = yf.astype(BF16)
            outs[1][0, sb] = yb.astype(BF16)
    h_ref[0] = hf
    h_ref[1] = hb
    if emit_state:
        @pl.when(i == nb - 1)
        def _():
            outs[-1][0] = h_ref[...]


def _ssd_call(cm, bm, xs, dt, h0, alog_pad, valid, *, emit_y, emit_state):
    b, l, _ = xs.shape
    nsub = min(SSD_BLOCK_CHUNKS, l // CHUNK)
    q = nsub * CHUNK
    nb = l // q

    def fwd(w):
        return pl.BlockSpec((1, q, w), lambda bi, i: (bi, i, 0))

    def bwd(w):
        return pl.BlockSpec((1, q, w), lambda bi, i: (bi, nb - 1 - i, 0))

    state_shape = (2, SSD_STATE, SSD_WIDTH)
    state_spec = pl.BlockSpec((1,) + state_shape, lambda bi, i: (bi, 0, 0, 0))
    out_specs, out_shape = [], []
    if emit_y:
        out_specs += [fwd(SSD_WIDTH), bwd(SSD_WIDTH)]
        out_shape += [jax.ShapeDtypeStruct((b, l, SSD_WIDTH), BF16)] * 2
    if emit_state:
        out_specs.append(state_spec)
        out_shape.append(jax.ShapeDtypeStruct((b,) + state_shape, F32))
    return pl.pallas_call(
        functools.partial(_ssd_kernel, nb=nb, nsub=nsub, emit_y=emit_y, emit_state=emit_state),
        grid=(b, nb),
        in_specs=[fwd(GROUP_STATE), fwd(GROUP_STATE), fwd(SSD_WIDTH), fwd(DT_PAD),
                  bwd(GROUP_STATE), bwd(GROUP_STATE), bwd(SSD_WIDTH), bwd(DT_PAD),
                  state_spec, _resident((2, HEAD_ROWS, CHUNK)), _resident((2, HEAD_ROWS, CHUNK))],
        out_specs=out_specs,
        out_shape=out_shape,
        scratch_shapes=[pltpu.VMEM(state_shape, F32)],
        compiler_params=pltpu.CompilerParams(dimension_semantics=("arbitrary", "arbitrary"),
                                             vmem_limit_bytes=VMEM_LIMIT),
        name="ssd_y" if emit_y else "ssd_state",
    )(cm, bm, xs, dt, cm, bm, xs, dt, h0, alog_pad, valid)


def _fourier_kernel(u_ref, g_ref, h_ref, m1_ref, w_ref, o_ref, y_scr, z_scr):
    r = DFT_R
    for l2 in range(r):
        slab = jnp.concatenate([u_ref[0, hf, pl.ds(l2, r, stride=r), :] for hf in range(2)], axis=1)
        y = _dot(g_ref[l2].astype(BF16), slab.astype(BF16))
        for hf in range(2):
            y_scr[hf, l2 * 2 * r:(l2 + 1) * 2 * r, :] = y[:, hf * LANES:(hf + 1) * LANES]
    hm = h_ref[...].astype(BF16)
    for k1 in range(r):
        parts = [jnp.concatenate([y_scr[hf, pl.ds(base + k1, r, stride=2 * r), :] for hf in range(2)], axis=1)
                 for base in (0, r)]
        zz = _dot(hm, jnp.concatenate(parts, axis=0).astype(BF16))
        for part in range(2):
            for hf in range(2):
                z_scr[2 * part + hf, pl.ds(k1, r, stride=r), :] = (
                    zz[part * r:(part + 1) * r, hf * LANES:(hf + 1) * LANES])
    zc = jnp.concatenate([z_scr[t] for t in range(4)], axis=1)
    f = _dot(zc.astype(BF16), m1_ref[...].astype(BF16))
    o_ref[0] = _dot(f.astype(BF16), w_ref[...]).astype(BF16)


def _fourier_call(uf, g_tab, h_tab, m1_tab, w_bd):
    b, _, l, _ = uf.shape
    w = FOURIER_WIDTH
    r = DFT_R
    return pl.pallas_call(
        _fourier_kernel,
        grid=(b,),
        in_specs=[pl.BlockSpec((1, 2, l, LANES), lambda bi: (bi, 0, 0, 0)),
                  _resident((r, 2 * r, r)), _resident((2 * r, 2 * r)), _resident((2 * w, w)),
                  _resident((w, w))],
        out_specs=pl.BlockSpec((1, l, w), lambda bi: (bi, 0, 0)),
        out_shape=jax.ShapeDtypeStruct((b, l, w), BF16),
        scratch_shapes=[pltpu.VMEM((2, 2 * l, LANES), F32), pltpu.VMEM((4, l, LANES), F32)],
        compiler_params=pltpu.CompilerParams(dimension_semantics=("arbitrary",),
                                             vmem_limit_bytes=VMEM_LIMIT),
        name="fourier",
    )(uf, g_tab, h_tab, m1_tab, w_bd)


def _dft_tables():
    r = DFT_R
    n = r * r
    l2 = np.arange(r)[:, None, None]
    k1 = np.arange(r)[None, :, None]
    l1 = np.arange(r)[None, None, :]
    ang = 2.0 * np.pi * ((k1 * (r * l1 + l2)) % n) / n
    g_tab = np.concatenate([np.cos(ang), -np.sin(ang)], axis=1)
    kk = np.arange(r)[:, None]
    ll = np.arange(r)[None, :]
    ang2 = 2.0 * np.pi * ((kk * ll) % r) / r
    c2, s2 = np.cos(ang2), np.sin(ang2)
    h_tab = np.block([[c2, s2], [-s2, c2]])
    heads = FOURIER_WIDTH // FOURIER_HEAD_DIM
    bd = lambda m: np.kron(np.eye(heads), m)
    scale = 1.0 / math.sqrt(n * FOURIER_HEAD_DIM)
    m1 = np.concatenate([bd(c2), bd(s2)], axis=0) * scale
    return (jnp.asarray(g_tab, F32), jnp.asarray(h_tab, F32), jnp.asarray(m1, F32))


def _mixout_kernel(yf_ref, yb_ref, xs_ref, z_ref, fo_ref, x_ref, mod_ref, dskip_ref, gssd_ref, gpost_ref,
                   wf_ref, wy_ref, o_ref):
    m = mod_ref[0]
    gate = m[5:6]
    y = yf_ref[0].astype(F32) + yb_ref[0].astype(F32) + dskip_ref[...] * xs_ref[0].astype(F32)
    y = y * _silu(z_ref[0].astype(F32))
    lane = lax.broadcasted_iota(jnp.int32, (1, SSD_WIDTH), 1)
    y2 = y * y
    rs = jnp.zeros_like(y)
    for g in range(SSD_GROUPS):
        mg = (lane >= g * GROUP_WIDTH) & (lane < (g + 1) * GROUP_WIDTH)
        ms = jnp.sum(jnp.where(mg, y2, 0.0), axis=-1, keepdims=True) * (1.0 / GROUP_WIDTH)
        rs = jnp.where(mg, lax.rsqrt(ms + EPS), rs)
    yn = (y * rs * gssd_ref[...]).astype(BF16)
    out = _dot(fo_ref[0], wf_ref[...]) + _dot(yn, wy_ref[...])
    o_ref[0] = x_ref[0] + gate * _rms(out, gpost_ref[...])


def _mixout_call(yf, yb, xs, z, fo, x, mod, dskip, g_ssd, g_post, w_f, w_y, tm):
    b, l, d = x.shape

    def tok(w):
        return pl.BlockSpec((1, tm, w), lambda bi, i: (bi, i, 0))

    return pl.pallas_call(
        _mixout_kernel,
        grid=(b, l // tm),
        in_specs=[tok(SSD_WIDTH), tok(SSD_WIDTH), tok(SSD_WIDTH), tok(SSD_WIDTH), tok(FOURIER_WIDTH), tok(d),
                  pl.BlockSpec((1, N_MOD, d), lambda bi, i: (bi, 0, 0)),
                  _resident((1, SSD_WIDTH)), _resident((1, SSD_WIDTH)), _resident((1, d)),
                  _resident((FOURIER_WIDTH, d)), _resident((SSD_WIDTH, d))],
        out_specs=tok(d),
        out_shape=jax.ShapeDtypeStruct((b, l, d), F32),
        compiler_params=pltpu.CompilerParams(dimension_semantics=("arbitrary", "arbitrary"),
                                             vmem_limit_bytes=VMEM_LIMIT),
        name="mixout",
    )(yf, yb, xs, z, fo, x, mod, dskip, g_ssd, g_post, w_f, w_y)


def _pad_heads(v):
    out = jnp.zeros((2, DT_PAD), F32)
    out = out.at[0, :SSD_HEADS].set(v[0])
    return out.at[1, DT_BWD_OFFSET:DT_BWD_OFFSET + SSD_HEADS].set(v[1])


def _head_rows(v):
    rows = jnp.zeros((2, HEAD_ROWS), F32).at[:, :SSD_HEADS].set(v)
    return jnp.broadcast_to(rows[:, :, None], (2, HEAD_ROWS, CHUNK))


def kernel(x, c, ctx, c_ctx, ada_w, ada_b, ffn1_norm_pre, ffn1_norm_post, ffn1_w_gate, ffn1_w_up, ffn1_w_down, mix_norm_pre, mix_norm_post, w_in, fourier_w, conv_w, conv_b, dt_bias, a_log, d_skip, ssd_norm, w_out, ffn2_norm_pre, ffn2_norm_post, ffn2_w_gate, ffn2_w_up, ffn2_w_down):
    b, l, d = x.shape
    lyr = 0
    ctx_row = b

    cc = jnp.zeros((16, d), F32).at[:b].set(c).at[b].set(c_ctx)
    mod = _mod_call(cc, ada_w[lyr], ada_b[lyr][None, :]).reshape(16, N_MOD, d)

    bf = lambda w: w.astype(BF16)
    row = lambda v: v[None, :]
    f1 = (row(ffn1_norm_pre[lyr]), row(ffn1_norm_post[lyr]), bf(ffn1_w_gate[lyr]), bf(ffn1_w_up[lyr]),
          bf(ffn1_w_down[lyr]))
    f2 = (row(ffn2_norm_pre[lyr]), row(ffn2_norm_post[lyr]), bf(ffn2_w_gate[lyr]), bf(ffn2_w_up[lyr]),
          bf(ffn2_w_down[lyr]))

    wi = w_in[lyr]
    n_main = COL_DT
    w_dt = jnp.zeros((d, DT_PAD), F32)
    w_dt = w_dt.at[:, :SSD_HEADS].set(wi[:, n_main:n_main + SSD_HEADS])
    w_dt = w_dt.at[:, DT_BWD_OFFSET:DT_BWD_OFFSET + SSD_HEADS].set(wi[:, n_main + SSD_HEADS:])
    w_pad = bf(jnp.concatenate([wi[:, :n_main], w_dt], axis=1))
    dtb_pad = _pad_heads(dt_bias[lyr])
    dtb_f = dtb_pad[0:1] + dtb_pad[1:2]
    alog_pad = _head_rows(a_log[lyr])
    valid = _head_rows(jnp.ones((2, SSD_HEADS), F32))
    conv_args = (conv_w[lyr], row(conv_b[lyr]), dtb_f)

    x1 = _ffn_call(x, mod, None, 0, *f1, tm=512)
    ctx1 = _ffn_call(ctx, mod, ctx_row, 0, *f1, tm=ctx.shape[1])

    zeros_state = jnp.zeros((b, 2, SSD_STATE, SSD_WIDTH), F32)
    _, _, _, xs_c, bm_c, dt_c = _inproj_call(ctx1, mod, ctx_row, row(mix_norm_pre[lyr]), w_pad, *conv_args,
                                             tm=ctx.shape[1])
    cm_dummy = bm_c
    (h_ctx,) = _ssd_call(cm_dummy, bm_c, xs_c, dt_c, zeros_state, alog_pad, valid,
                         emit_y=False, emit_state=True)

    uf, z, cm, xs, bm, dt = _inproj_call(x1, mod, None, row(mix_norm_pre[lyr]), w_pad, *conv_args, tm=512)
    yf, yb = _ssd_call(cm, bm, xs, dt, h_ctx, alog_pad, valid, emit_y=True, emit_state=False)
    g_tab, h_tab, m1_tab = _dft_tables()
    heads = FOURIER_WIDTH // FOURIER_HEAD_DIM
    w_bd = jnp.zeros((FOURIER_WIDTH, FOURIER_WIDTH), F32)
    for hd in range(heads):
        s = slice(hd * FOURIER_HEAD_DIM, (hd + 1) * FOURIER_HEAD_DIM)
        w_bd = w_bd.at[s, s].set(fourier_w[lyr, hd])
    fo = _fourier_call(uf, g_tab, h_tab, m1_tab, bf(w_bd))
    dskip = jnp.repeat(d_skip[lyr], SSD_HEAD_DIM)[None, :]
    wo = bf(w_out[lyr])
    x2 = _mixout_call(yf, yb, xs, z, fo, x1, mod, dskip, row(ssd_norm[lyr]), row(mix_norm_post[lyr]),
                      wo[:FOURIER_WIDTH], wo[FOURIER_WIDTH:], tm=512)

    return _ffn_call(x2, mod, None, 6, *f2, tm=512)
```
